```python
import jax
import jax.numpy as jnp
from jax import lax
import numpy as np

D_MODEL = 1024
BATCH = 2
SEQ = 8192
DEPTH = 4
DEC_BATCH = 128
DEC_SEQ = 8
PAST_LEN = 8192
PAGE_SIZE = 128

HEAD_DIM = 64
GROUP_WIDTH = D_MODEL // 4
A_HEADS = GROUP_WIDTH // HEAD_DIM
CHUNK = 128
B_HEADS = GROUP_WIDTH // HEAD_DIM
B_KV = 2
B_WINDOW = 128
BLOCK = 128
C_PAIRS = ((128, 1), (512, 4), (2048, 16))
C_NB = len(C_PAIRS)
C_HEADS = GROUP_WIDTH // HEAD_DIM
C_KV = 2
CONV_W = 31
D_CH = GROUP_WIDTH
D_GROUPS = 4
ROPE_THETA = 500000.0
ROT_DIM = HEAD_DIM // 4
PEER_HEADS = 8
PEER_NKEYS = 128
PEER_EXPERTS = PEER_NKEYS * PEER_NKEYS
PEER_TOPK = 16
PEER_KDIM = 128
PEER_BLOCK = 128
EPS = 1e-6
NEG = -1e30
ATTN_SCALE = HEAD_DIM ** -0.5
PROJ_SIZES = (GROUP_WIDTH, GROUP_WIDTH,
              B_HEADS * HEAD_DIM, B_KV * HEAD_DIM, B_KV * HEAD_DIM,
              C_NB * C_HEADS * HEAD_DIM, C_NB * C_KV * HEAD_DIM, C_NB * C_KV * HEAD_DIM,
              D_CH, D_CH)
PROJ_WIDTH = sum(PROJ_SIZES)

kernel_name = 'hybrid_parallel_group_decoder_step'


def _rms(x):
    xf = x.astype(jnp.float32)
    return xf * lax.rsqrt(jnp.mean(xf * xf, -1, keepdims=True) + EPS)


def _rmsnorm(x, g):
    return (_rms(x) * g.astype(jnp.float32)).astype(x.dtype)


def _rope(x, pos):
    half = ROT_DIM // 2
    inv = ROPE_THETA ** (-jnp.arange(half, dtype=jnp.float32) * 2.0 / ROT_DIM)
    ang = pos.astype(jnp.float32)[:, None] * inv[None, :]
    cos, sin = jnp.cos(ang)[:, None, :], jnp.sin(ang)[:, None, :]
    xf = x[..., :ROT_DIM].astype(jnp.float32)
    x1, x2 = xf[..., :half], xf[..., half:]
    rot = jnp.concatenate([x1 * cos - x2 * sin, x2 * cos + x1 * sin], -1).astype(x.dtype)
    return jnp.concatenate([rot, x[..., ROT_DIM:]], -1)


def _split_proj(p):
    offs = np.cumsum(PROJ_SIZES)[:-1].tolist()
    return jnp.split(p, offs, axis=-1)


def _chunk_spatial_gate(u, v, ws, bs, L):
    N, T, H, Dh = v.shape
    wm = jnp.where(jnp.tril(jnp.ones((L, L), bool)), ws[:, :L, :L], 0)
    vc = v.reshape(N, T // L, L, H, Dh)
    z = jnp.einsum('hij,ncjhd->ncihd', wm, vc) + bs[:, :L].T[None, None, :, :, None]
    return u * z.reshape(N, T, H * Dh)


def _sink_attend(s, mask, sink, v, eq):
    s = jnp.where(mask, s, NEG)
    m = jnp.maximum(s.max(-1), sink)
    p = jnp.exp(s - m[..., None])
    den = p.sum(-1) + jnp.exp(sink - m)
    return jnp.einsum(eq, (p / den[..., None]).astype(v.dtype), v)


def _swa_sink_prompt(q, k, v, sink):
    N, S, H, Dh = q.shape
    KV = k.shape[2]
    G = H // KV
    nb = S // BLOCK
    qb = q.reshape(N, nb, BLOCK, KV, G, Dh)

    def bands(t):
        cur = t.reshape(N, nb, BLOCK, KV, Dh)
        prev = jnp.pad(t, ((0, 0), (BLOCK, 0), (0, 0), (0, 0)))[:, :S].reshape(N, nb, BLOCK, KV, Dh)
        return jnp.concatenate([prev, cur], axis=2)

    kb, vb = bands(k), bands(v)
    s = jnp.einsum('nbqkgd,nbskd->nbkgqs', qb, kb).astype(jnp.float32) * ATTN_SCALE
    i = jnp.arange(BLOCK)[:, None]
    j = jnp.arange(2 * BLOCK)[None, :]
    dist = i + BLOCK - j
    kpos = jnp.arange(nb)[:, None] * BLOCK - BLOCK + jnp.arange(2 * BLOCK)[None, :]
    mask = ((dist >= 0) & (dist <= B_WINDOW))[None] & (kpos >= 0)[:, None, :]
    sk = sink.astype(jnp.float32).reshape(KV, G, 1)
    o = _sink_attend(s, mask[None, :, None, None], sk, vb, 'nbkgqs,nbskd->nbqkgd')
    return o.reshape(N, S, H * Dh)


def _swa_sink_sample(q, k, v, cache, sink, pos):
    N, T, H, Dh = q.shape
    KV = k.shape[2]
    G = H // KV
    L = cache.shape[1]
    kall = jnp.concatenate([cache[:, :, 0], k], 1)
    vall = jnp.concatenate([cache[:, :, 1], v], 1)
    kpos = jnp.concatenate([PAST_LEN - L + jnp.arange(L), pos])
    dist = pos[:, None] - kpos[None, :]
    mask = (dist >= 0) & (dist <= B_WINDOW)
    s = jnp.einsum('ntkgd,nskd->nkgts', q.reshape(N, T, KV, G, Dh), kall).astype(jnp.float32) * ATTN_SCALE
    sk = sink.astype(jnp.float32).reshape(KV, G, 1)
    o = _sink_attend(s, mask, sk, vall, 'nkgts,nskd->ntkgd')
    return o.reshape(N, T, H * Dh)


def _dilated_attend(q, k, v, idx, valid):
    N, Tq, H, Dh = q.shape
    KV = k.shape[2]
    G = H // KV
    kg, vg = k[:, idx], v[:, idx]
    s = jnp.einsum('ntkgd,ntjkd->ntkgj', q.reshape(N, Tq, KV, G, Dh), kg).astype(jnp.float32) * ATTN_SCALE
    s = jnp.where(valid[None, :, None, None, :], s, NEG)
    m = s.max(-1, keepdims=True)
    p = jnp.exp(s - m)
    den = p.sum(-1, keepdims=True)
    o = jnp.einsum('ntkgj,ntjkd->ntkgd', (p / den).astype(v.dtype), vg)
    lse = (m + jnp.log(den))[..., 0]
    return o.reshape(N, Tq, H, Dh), lse.reshape(N, Tq, H)


def _combine_branches(outs, lses, dtype):
    alpha = jax.nn.softmax(jnp.stack(lses), axis=0)[..., None]
    return jnp.sum(alpha * jnp.stack(outs).astype(jnp.float32), 0).astype(dtype)


def _dilated_prompt(q, k, v):
    N, S = q.shape[:2]
    nb = S // BLOCK

    def block(b):
        qpos = b * BLOCK + jnp.arange(BLOCK)
        qb = lax.dynamic_slice_in_dim(q, b * BLOCK, BLOCK, axis=1)
        outs, lses = [], []
        for g, (win, dil) in enumerate(C_PAIRS):
            kp = qpos[:, None] - dil * jnp.arange(win // dil + 1)[None, :]
            o, lse = _dilated_attend(qb[:, :, g], k[:, :, g], v[:, :, g], jnp.maximum(kp, 0), kp >= 0)
            outs.append(o)
            lses.append(lse)
        return _combine_branches(outs, lses, q.dtype)

    y = lax.map(block, jnp.arange(nb))
    return jnp.moveaxis(y, 0, 1).reshape(N, S, C_HEADS * HEAD_DIM)


def _dilated_sample(q, k, v, caches, pos):
    N, T = q.shape[:2]
    outs, lses = [], []
    for g, (win, dil) in enumerate(C_PAIRS):
        cache = caches[g]
        L = cache.shape[1]
        kall = jnp.concatenate([cache[:, :, 0], k[:, :, g]], 1)
        vall = jnp.concatenate([cache[:, :, 1], v[:, :, g]], 1)
        kp = pos[:, None] - dil * jnp.arange(win // dil + 1)[None, :]
        ci = kp - (PAST_LEN - L)
        o, lse = _dilated_attend(q[:, :, g], kall, vall, jnp.maximum(ci, 0), ci >= 0)
        outs.append(o)
        lses.append(lse)
    return _combine_branches(outs, lses, q.dtype).reshape(N, T, C_HEADS * HEAD_DIM)


def _conv_module(conv_in, w, b, gn):
    C = conv_in.shape[-1]
    y = lax.conv_general_dilated(conv_in, w[:, None, :], (1,), 'VALID',
                                 dimension_numbers=('NWC', 'WIO', 'NWC'), feature_group_count=C) + b
    N, T, _ = y.shape
    yn = _rms(y.reshape(N, T, D_GROUPS, C // D_GROUPS)).reshape(N, T, C) * gn.astype(jnp.float32)
    return jax.nn.silu(yn).astype(conv_in.dtype)


def _peer(h, wq, subkeys, U, V):
    N, T, D = h.shape
    M = N * T
    nblk = -(-M // PEER_BLOCK)
    xt = jnp.pad(h.reshape(M, D), ((0, nblk * PEER_BLOCK - M), (0, 0))).reshape(nblk, PEER_BLOCK, D)
    half = PEER_KDIM // 2

    def route(xb):
        q = (xb @ wq).reshape(PEER_BLOCK, PEER_HEADS, PEER_KDIM)
        s1 = jnp.einsum('bhd,nd->bhn', q[..., :half], subkeys[0]).astype(jnp.float32)
        s2 = jnp.einsum('bhd,nd->bhn', q[..., half:], subkeys[1]).astype(jnp.float32)
        t1, i1 = lax.top_k(s1, PEER_TOPK)
        t2, i2 = lax.top_k(s2, PEER_TOPK)
        cand = (t1[..., :, None] + t2[..., None, :]).reshape(PEER_BLOCK, PEER_HEADS, PEER_TOPK * PEER_TOPK)
        tv, ti = lax.top_k(cand, PEER_TOPK)
        e = (jnp.take_along_axis(i1, ti // PEER_TOPK, -1) * PEER_NKEYS
             + jnp.take_along_axis(i2, ti % PEER_TOPK, -1))
        g = jax.nn.softmax(tv, -1)
        act = jax.nn.gelu(jnp.einsum('bd,bhkd->bhk', xb, U[e]).astype(jnp.float32))
        return jnp.einsum('bhk,bhkd->bd', (g * act).astype(xb.dtype), V[e])

    y = lax.map(route, xt).reshape(nblk * PEER_BLOCK, D)[:M]
    return y.reshape(N, T, D)


def _token_mixers(h, pos, is_prompt, cache, w):
    N, T, _ = h.shape
    a_u, a_v, b_q, b_k, b_v, c_q, c_k, c_v, d_a, d_g = _split_proj(h @ w['w_in'])
    av = _rmsnorm(jax.nn.gelu(a_v), w['g_a_v']).reshape(N, T, A_HEADS, HEAD_DIM)
    y_a = _chunk_spatial_gate(jax.nn.gelu(a_u), av, w['a_ws'], w['a_bs'], CHUNK if is_prompt else T)
    bq = _rope(_rmsnorm(b_q.reshape(N, T, B_HEADS, HEAD_DIM), w['b_gq']), pos)
    bk = _rope(_rmsnorm(b_k.reshape(N, T, B_KV, HEAD_DIM), w['b_gk']), pos)
    bv = b_v.reshape(N, T, B_KV, HEAD_DIM)
    cq = _rmsnorm(c_q.reshape(N, T, C_NB, C_HEADS, HEAD_DIM), w['c_gq'][:, None, :])
    ck = _rmsnorm(c_k.reshape(N, T, C_NB, C_KV, HEAD_DIM), w['c_gk'][:, None, :])
    cq = _rope(cq.reshape(N, T, C_NB * C_HEADS, HEAD_DIM), pos).reshape(N, T, C_NB, C_HEADS, HEAD_DIM)
    ck = _rope(ck.reshape(N, T, C_NB * C_KV, HEAD_DIM), pos).reshape(N, T, C_NB, C_KV, HEAD_DIM)
    cv = c_v.reshape(N, T, C_NB, C_KV, HEAD_DIM)
    glu = d_a * jax.nn.sigmoid(d_g)
    if is_prompt:
        y_b = _swa_sink_prompt(bq, bk, bv, w['b_sink'])
        y_c = _dilated_prompt(cq, ck, cv)
        conv_in = jnp.pad(glu, ((0, 0), (CONV_W - 1, 0), (0, 0)))
        b_new = jnp.stack([bk, bv], 2)[:, -min(B_WINDOW, T):]
        c_new = tuple(jnp.stack([ck[:, :, g], cv[:, :, g]], 2)[:, -min(win, T):]
                      for g, (win, _) in enumerate(C_PAIRS))
        a_new = None
    else:
        cache_b, cache_c1, cache_c2, cache_c3, state_d = cache
        y_b = _swa_sink_sample(bq, bk, bv, cache_b, w['b_sink'], pos)
        y_c = _dilated_sample(cq, ck, cv, (cache_c1, cache_c2, cache_c3), pos)
        conv_in = jnp.concatenate([state_d, glu], 1)
        b_new = jnp.stack([bk, bv], 2)
        c_new = tuple(jnp.stack([ck[:, :, g], cv[:, :, g]], 2) for g in range(C_NB))
        a_new = av
    y_d = _conv_module(conv_in, w['d_conv_w'], w['d_conv_b'], w['d_gn'])
    d_new = conv_in[:, -(CONV_W - 1):]
    mix = jnp.concatenate([y_a, y_b, y_c, y_d], -1)
    return mix, (b_new, c_new[0], c_new[1], c_new[2], d_new, a_new)


def _layer(x, c, pos, is_prompt, cache, w):
    mod = jax.nn.silu(c) @ w['w_ada'] + w['b_ada']
    sh1, sc1, g1, sh2, sc2, g2 = jnp.split(mod[:, None, :], 6, axis=-1)
    h = _rmsnorm(x, w['g_mix']) * (1 + sc1) + sh1
    mix, new_state = _token_mixers(h, pos, is_prompt, cache, w)
    x = x + g1 * (mix @ w['w_out'])
    h2 = _rmsnorm(x, w['g_ffn']) * (1 + sc2) + sh2
    x = x + g2 * _peer(h2, w['peer_wq'], w['peer_subkeys'], w['peer_u'], w['peer_v'])
    return x, new_state


def setup_inputs(seed: int = 0) -> dict:
    key = jax.random.key(seed)
    ks = iter(jax.random.split(key, 40))

    def nrm(shape, s):
        return jax.random.normal(next(ks), shape, jnp.float32) * s

    def gain(shape):
        return 1.0 + nrm(shape, 0.02)

    lb = min(B_WINDOW, PAST_LEN)
    lc = [min(win, PAST_LEN) for win, _ in C_PAIRS]
    return {
        'x_prompt': nrm((BATCH, SEQ, D_MODEL), 1.0),
        'x_sample': nrm((DEC_BATCH, DEC_SEQ, D_MODEL), 1.0),
        'cache_b_kv': nrm((DEPTH, DEC_BATCH, lb, 2, B_KV, HEAD_DIM), 1.0),
        'cache_c1_kv': nrm((DEPTH, DEC_BATCH, lc[0], 2, C_KV, HEAD_DIM), 1.0),
        'cache_c2_kv': nrm((DEPTH, DEC_BATCH, lc[1], 2, C_KV, HEAD_DIM), 1.0),
        'cache_c3_kv': nrm((DEPTH, DEC_BATCH, lc[2], 2, C_KV, HEAD_DIM), 1.0),
        'state_d_conv': nrm((DEPTH, DEC_BATCH, CONV_W - 1, D_CH), 0.5),
        'c_prompt': nrm((BATCH, D_MODEL), 1.0),
        'c_sample': nrm((DEC_BATCH, D_MODEL), 1.0),
        'w_ada': nrm((DEPTH, D_MODEL, 6 * D_MODEL), 0.5 * D_MODEL ** -0.5),
        'b_ada': nrm((DEPTH, 6 * D_MODEL), 0.02),
        'g_mix': gain((DEPTH, D_MODEL)),
        'w_in': nrm((DEPTH, D_MODEL, PROJ_WIDTH), D_MODEL ** -0.5),
        'w_out': nrm((DEPTH, D_MODEL, D_MODEL), D_MODEL ** -0.5),
        'g_a_v': gain((DEPTH, GROUP_WIDTH)),
        'a_ws': nrm((DEPTH, A_HEADS, CHUNK, CHUNK), CHUNK ** -0.5),
        'a_bs': 1.0 + nrm((DEPTH, A_HEADS, CHUNK), 0.1),
        'b_gq': gain((DEPTH, HEAD_DIM)),
        'b_gk': gain((DEPTH, HEAD_DIM)),
        'b_sink': nrm((DEPTH, B_HEADS), 0.5),
        'c_gq': gain((DEPTH, C_NB, HEAD_DIM)),
        'c_gk': gain((DEPTH, C_NB, HEAD_DIM)),
        'd_conv_w': nrm((DEPTH, CONV_W, D_CH), CONV_W ** -0.5),
        'd_conv_b': nrm((DEPTH, D_CH), 0.02),
        'd_gn': gain((DEPTH, D_CH)),
        'g_ffn': gain((DEPTH, D_MODEL)),
        'peer_wq': nrm((DEPTH, D_MODEL, PEER_HEADS * PEER_KDIM), D_MODEL ** -0.5),
        'peer_subkeys': nrm((DEPTH, 2, PEER_NKEYS, PEER_KDIM // 2), (PEER_KDIM // 2) ** -0.5),
        'peer_u': nrm((DEPTH, PEER_EXPERTS, D_MODEL), D_MODEL ** -0.5),
        'peer_v': nrm((DEPTH, PEER_EXPERTS, D_MODEL), 0.5),
    }


def reference(x_prompt, x_sample, cache_b_kv, cache_c1_kv, cache_c2_kv, cache_c3_kv, state_d_conv,
              c_prompt, c_sample, w_ada, b_ada, g_mix, w_in, w_out, g_a_v, a_ws, a_bs, b_gq, b_gk,
              b_sink, c_gq, c_gk, d_conv_w, d_conv_b, d_gn, g_ffn, peer_wq, peer_subkeys, peer_u, peer_v):
    pos_p = jnp.arange(x_prompt.shape[1], dtype=jnp.int32)
    pos_s = PAST_LEN + jnp.arange(x_sample.shape[1], dtype=jnp.int32)
    yp, ys = x_prompt, x_sample
    st_p, st_s = [], []
    for l in range(DEPTH):
        w = dict(w_ada=w_ada[l], b_ada=b_ada[l], g_mix=g_mix[l], w_in=w_in[l], w_out=w_out[l],
                 g_a_v=g_a_v[l], a_ws=a_ws[l], a_bs=a_bs[l], b_gq=b_gq[l], b_gk=b_gk[l],
                 b_sink=b_sink[l], c_gq=c_gq[l], c_gk=c_gk[l], d_conv_w=d_conv_w[l],
                 d_conv_b=d_conv_b[l], d_gn=d_gn[l], g_ffn=g_ffn[l], peer_wq=peer_wq[l],
                 peer_subkeys=peer_subkeys[l], peer_u=peer_u[l], peer_v=peer_v[l])
        yp, sp = _layer(yp, c_prompt, pos_p, True, None, w)
        cache_l = (cache_b_kv[l], cache_c1_kv[l], cache_c2_kv[l], cache_c3_kv[l], state_d_conv[l])
        ys, ss = _layer(ys, c_sample, pos_s, False, cache_l, w)
        st_p.append(sp)
        st_s.append(ss)
    b_p, c1_p, c2_p, c3_p, d_p = (jnp.stack([s[i] for s in st_p]) for i in range(5))
    b_s, c1_s, c2_s, c3_s, d_s, a_s = (jnp.stack([s[i] for s in st_s]) for i in range(6))
    return (yp, ys, b_p, b_s, c1_p, c1_s, c2_p, c2_s, c3_p, c3_s, d_p, d_s, a_s)
```

```python
import functools

import jax
import jax.numpy as jnp
import numpy as np
from jax import lax
from jax.experimental import pallas as pl
from jax.experimental.pallas import tpu as pltpu

F32 = jnp.float32
BF16 = jnp.bfloat16

D_MODEL = 1024
DEPTH = 4
PAST_LEN = 8192
HEAD_DIM = 64
GROUP_WIDTH = 256
CHUNK = 128
BLOCK = 128
C_PAIRS = ((128, 1), (512, 4), (2048, 16))
CONV_W = 31
ROPE_THETA = 500000.0
ROT_DIM = 16
PEER_HEADS = 8
PEER_NKEYS = 128
PEER_TOPK = 16
EPS = 1e-6
NEG = -1e30
ATTN_SCALE = HEAD_DIM ** -0.5

LANES = 128
SUBLANES = 8
VMEM_LIMIT = 56 * 1024 * 1024

P_AU, P_AV, P_Q, P_K, P_V, P_DA, P_DG, P_END = 0, 256, 512, 1536, 2560, 3584, 3840, 4096


def _cparams(sem):
    return pltpu.CompilerParams(dimension_semantics=sem, vmem_limit_bytes=VMEM_LIMIT)


def _gelu(x):
    return 0.5 * x * (1.0 + jnp.tanh(0.7978845608028654 * (x + 0.044715 * (x * x * x))))


def _sigmoid(x):
    return 1.0 / (1.0 + jnp.exp(-x))


def _dot(a, b):
    return jnp.dot(a, b, preferred_element_type=F32)


def _group_mean(sq, gm):
    hi = sq.astype(BF16)
    lo = (sq - hi.astype(F32)).astype(BF16)
    return _dot(hi, gm) + _dot(lo, gm)


def _lane_lt64(shape):
    return lax.broadcasted_iota(jnp.int32, shape, len(shape) - 1) % LANES < HEAD_DIM


def _ada_kernel(c_ref, w_ref, b_ref, o_ref):
    c = c_ref[...]
    s = (c * _sigmoid(c)).astype(BF16)
    o_ref[0] = _dot(s, w_ref[0]) + b_ref[0]


def _ada_call(c_all, w_ada, b_ada):
    n = c_all.shape[0]
    ncol = w_ada.shape[-1] // D_MODEL
    return pl.pallas_call(
        _ada_kernel,
        out_shape=jax.ShapeDtypeStruct((DEPTH, n, ncol * D_MODEL), F32),
        grid=(DEPTH, ncol),
        in_specs=[
            pl.BlockSpec((n, D_MODEL), lambda l, j: (0, 0)),
            pl.BlockSpec((1, D_MODEL, D_MODEL), lambda l, j: (l, 0, j)),
            pl.BlockSpec((1, 1, D_MODEL), lambda l, j: (l, 0, j)),
        ],
        out_specs=pl.BlockSpec((1, n, D_MODEL), lambda l, j: (l, 0, j)),
        compiler_params=_cparams(("arbitrary", "arbitrary")),
    )(c_all, w_ada, b_ada)


def _tile_maps(t_len, tt):
    ntt = t_len // tt

    def tok(c):
        return lambda i: (i // ntt, i % ntt, c)

    def mod(c):
        return lambda i: (i // ntt, 0, c)

    return ntt, tok, mod


def _norm_mod(x, g, sc, sh):
    ms = jnp.mean(x * x, axis=-1, keepdims=True)
    return (x * lax.rsqrt(ms + EPS) * g) * (1.0 + sc) + sh


def _pre_kernel(x_ref, sh_ref, sc_ref, gmix_ref, win_ref, gav_ref, mixw_ref, mixb_ref, hmask_ref,
                gq_ref, gk_ref, gm_ref, cc_ref, sa_ref, sb_ref,
                ya_ref, av_ref, q_ref, kd_ref, vd_ref, kvc_ref, glu_ref, *, nb, tt):
    tm = nb * tt
    h = _norm_mod(x_ref[...], gmix_ref[...], sc_ref[...], sh_ref[...])
    hb = h.reshape(tm, D_MODEL).astype(BF16)

    def proj(a, b):
        return _dot(hb, win_ref[:, a:b])

    def put(ref, a, b, val):
        ref[:, :, a:b] = val.reshape(nb, tt, b - a).astype(ref.dtype)

    au = _gelu(proj(P_AU, P_AV))
    avr = _gelu(proj(P_AV, P_Q))
    av = avr * lax.rsqrt(jnp.mean(avr * avr, axis=-1, keepdims=True) + EPS) * gav_ref[...]
    put(av_ref, 0, GROUP_WIDTH, av)
    for s in range(tm // CHUNK):
        sub = av[s * CHUNK:(s + 1) * CHUNK]
        rhs = jnp.concatenate([sub * hmask_ref[hh:hh + 1, :] for hh in range(4)], axis=0).astype(BF16)
        z = _dot(mixw_ref[...], rhs) + mixb_ref[...]
        ya_tile = au[s * CHUNK:(s + 1) * CHUNK] * z
        if tt >= CHUNK:
            ya_ref[0, s * CHUNK:(s + 1) * CHUNK, :] = ya_tile
        else:
            per = CHUNK // tt
            ya_ref[s * per:(s + 1) * per, :, :] = ya_tile.reshape(per, tt, GROUP_WIDTH)

    gm = gm_ref[...]
    cc, sa, sb = cc_ref[...], sa_ref[...], sb_ref[...]

    def norm_rope(xc, gain):
        xn = xc * lax.rsqrt(_group_mean(xc * xc, gm) + EPS) * gain
        up = pltpu.roll(xn, GROUP_WIDTH - ROT_DIM // 2, 1)
        dn = pltpu.roll(xn, ROT_DIM // 2, 1)
        return xn * cc + up * sa + dn * sb

    lt64 = _lane_lt64((tm, LANES))
    for j in range(4):
        a = P_Q + j * GROUP_WIDTH
        qn = norm_rope(proj(a, a + GROUP_WIDTH), gq_ref[:, j * GROUP_WIDTH:(j + 1) * GROUP_WIDTH])
        put(q_ref, j * GROUP_WIDTH, (j + 1) * GROUP_WIDTH, qn * ATTN_SCALE)
        a = P_K + j * GROUP_WIDTH
        kn = norm_rope(proj(a, a + GROUP_WIDTH), gk_ref[:, j * GROUP_WIDTH:(j + 1) * GROUP_WIDTH])
        put(kd_ref, j * GROUP_WIDTH, (j + 1) * GROUP_WIDTH, kn)
        a = P_V + j * GROUP_WIDTH
        vn = proj(a, a + GROUP_WIDTH)
        put(vd_ref, j * GROUP_WIDTH, (j + 1) * GROUP_WIDTH, vn)
        put(kvc_ref, j * GROUP_WIDTH, j * GROUP_WIDTH + LANES, jnp.where(lt64, kn[:, :LANES], kn[:, LANES:]))
        put(kvc_ref, j * GROUP_WIDTH + LANES, (j + 1) * GROUP_WIDTH,
            jnp.where(lt64, vn[:, :LANES], vn[:, LANES:]))

    put(glu_ref, 0, GROUP_WIDTH, proj(P_DA, P_DG) * _sigmoid(proj(P_DG, P_END)))


def _pre_call(x, mod, lw, rope, *, nb, tt, rope_by_tile):
    n, t_len, _ = x.shape
    ntt, tok, modm = _tile_maps(t_len, tt)
    tm = nb * tt
    grid = ((n // nb) * ntt,)
    full2 = lambda i: (0, 0)
    rope_map = (lambda i: (i % ntt, 0)) if rope_by_tile else full2
    cst = lambda shape: pl.BlockSpec(shape, full2)
    in_specs = [
        pl.BlockSpec((nb, tt, D_MODEL), tok(0)),
        pl.BlockSpec((nb, 1, D_MODEL), modm(0)),
        pl.BlockSpec((nb, 1, D_MODEL), modm(1)),
        cst((1, D_MODEL)),
        cst((D_MODEL, P_END)),
        cst((1, GROUP_WIDTH)),
        cst((CHUNK, 4 * CHUNK)),
        cst((CHUNK, GROUP_WIDTH)),
        cst((4, GROUP_WIDTH)),
        cst((1, 4 * GROUP_WIDTH)),
        cst((1, 4 * GROUP_WIDTH)),
        cst((GROUP_WIDTH, GROUP_WIDTH)),
        pl.BlockSpec((tm, GROUP_WIDTH), rope_map),
        pl.BlockSpec((tm, GROUP_WIDTH), rope_map),
        pl.BlockSpec((tm, GROUP_WIDTH), rope_map),
    ]
    widths = (GROUP_WIDTH, GROUP_WIDTH, 1024, 1024, 1024, 1024, GROUP_WIDTH)
    dtypes = (F32, F32, F32, BF16, BF16, F32, F32)
    out_shape = tuple(jax.ShapeDtypeStruct((n, t_len, w), dt) for w, dt in zip(widths, dtypes))
    out_specs = tuple(pl.BlockSpec((nb, tt, w), tok(0)) for w in widths)
    return pl.pallas_call(
        functools.partial(_pre_kernel, nb=nb, tt=tt),
        out_shape=out_shape, grid=grid, in_specs=in_specs, out_specs=out_specs,
        compiler_params=_cparams(("arbitrary",)),
    )(x, mod, mod, lw["g_mix"], lw["w_in"], lw["g_a_v"], lw["mixw"], lw["mixb"], lw["hmask"],
      lw["gq"], lw["gk"], lw["gm"], rope[0], rope[1], rope[2])


def _band_attn_kernel(sink_ref, q_ref, kp_ref, kc_ref, vp_ref, vc_ref, o_ref, lse_ref, *, tq):
    has_prev = pl.program_id(1) > 0
    row = lax.broadcasted_iota(jnp.int32, (2 * BLOCK, 2 * BLOCK), 0) % BLOCK
    col = lax.broadcasted_iota(jnp.int32, (2 * BLOCK, 2 * BLOCK), 1)
    band = (col >= row) & (col <= row + BLOCK)
    first_band = band & ((col >= BLOCK) | has_prev)
    lt64 = _lane_lt64((BLOCK, LANES))
    top = lax.broadcasted_iota(jnp.int32, (2 * BLOCK, 1), 0) < BLOCK
    for s in range(tq // BLOCK):
        for h in range(2):
            ls = slice(h * LANES, (h + 1) * LANES)
            q = q_ref[0, s * BLOCK:(s + 1) * BLOCK, ls]
            q2 = jnp.concatenate([jnp.where(lt64, q, 0.0), jnp.where(lt64, 0.0, q)], axis=0).astype(BF16)
            if s == 0:
                k = jnp.concatenate([kp_ref[0, :, ls], kc_ref[0, 0:BLOCK, ls]], axis=0)
                v = jnp.concatenate([vp_ref[0, :, ls], vc_ref[0, 0:BLOCK, ls]], axis=0)
                mask = first_band
            else:
                k = kc_ref[0, (s - 1) * BLOCK:(s + 1) * BLOCK, ls]
                v = vc_ref[0, (s - 1) * BLOCK:(s + 1) * BLOCK, ls]
                mask = band
            sc = lax.dot_general(q2, k, (((1,), (1,)), ((), ())), preferred_element_type=F32)
            sc = jnp.where(mask, sc, NEG)
            sink = jnp.where(top, sink_ref[2 * h], sink_ref[2 * h + 1])
            m = jnp.maximum(jnp.max(sc, axis=-1, keepdims=True), sink)
            p = jnp.exp(sc - m)
            den = jnp.sum(p, axis=-1, keepdims=True) + jnp.exp(sink - m)
            o2 = _dot(p.astype(BF16), v) / den
            lse = jnp.broadcast_to(m + jnp.log(den), (2 * BLOCK, LANES))
            o_ref[0, s * BLOCK:(s + 1) * BLOCK, ls] = jnp.where(lt64, o2[:BLOCK], o2[BLOCK:])
            lse_ref[0, s * BLOCK:(s + 1) * BLOCK, ls] = jnp.where(lt64, lse[:BLOCK], lse[BLOCK:])


def _band_attn_call(sink, q, kd, vd, col, *, tq):
    b, l_len, _ = q.shape
    per = tq // BLOCK
    cur = lambda bb, i: (bb, i, col)
    prev = lambda bb, i: (bb, jnp.maximum(i * per - 1, 0), col)
    out = jax.ShapeDtypeStruct((b, l_len, GROUP_WIDTH), F32)
    return pl.pallas_call(
        functools.partial(_band_attn_kernel, tq=tq),
        out_shape=(out, out),
        grid=(b, l_len // tq),
        in_specs=[
            pl.BlockSpec(memory_space=pltpu.SMEM),
            pl.BlockSpec((1, tq, GROUP_WIDTH), cur),
            pl.BlockSpec((1, BLOCK, GROUP_WIDTH), prev),
            pl.BlockSpec((1, tq, GROUP_WIDTH), cur),
            pl.BlockSpec((1, BLOCK, GROUP_WIDTH), prev),
            pl.BlockSpec((1, tq, GROUP_WIDTH), cur),
        ],
        out_specs=(pl.BlockSpec((1, tq, GROUP_WIDTH), lambda bb, i: (bb, i, 0)),) * 2,
        compiler_params=_cparams(("arbitrary", "arbitrary")),
    )(sink, q, kd, kd, vd, vd)


def _sample_attn_kernel(sink_ref, q_ref, kvn_ref, cb_ref, c1_ref, c2_ref, c3_ref, yb_ref, yc_ref, *, t_new):
    lt64 = _lane_lt64((t_new, LANES))
    rows = 4 * t_new
    trow = lax.broadcasted_iota(jnp.int32, (rows, 1), 0) % t_new
    hrow = lax.broadcasted_iota(jnp.int32, (rows, 1), 0) // t_new
    caches = (cb_ref, c1_ref, c2_ref, c3_ref)
    dils = (1,) + tuple(d for _, d in C_PAIRS)
    outs, lses = [], []
    for j in range(4):
        dil = dils[j]
        cache = caches[j]
        l_len = cache.shape[1]
        qa = q_ref[0, :, j * GROUP_WIDTH:j * GROUP_WIDTH + LANES]
        qb = q_ref[0, :, j * GROUP_WIDTH + LANES:(j + 1) * GROUP_WIDTH]
        qst = jnp.concatenate([
            jnp.where(lt64, qa, 0.0),
            jnp.where(lt64, pltpu.roll(qa, HEAD_DIM, 1), 0.0),
            jnp.where(lt64, 0.0, pltpu.roll(qb, HEAD_DIM, 1)),
            jnp.where(lt64, 0.0, qb)], axis=0).astype(BF16)
        kc = cache[0, :, 0:LANES].astype(BF16)
        vc = cache[0, :, LANES:2 * LANES].astype(BF16)
        kn = kvn_ref[0, :, j * GROUP_WIDTH:j * GROUP_WIDTH + LANES].astype(BF16)
        vn = kvn_ref[0, :, j * GROUP_WIDTH + LANES:(j + 1) * GROUP_WIDTH].astype(BF16)
        nt = (((1,), (1,)), ((), ()))
        s_c = lax.dot_general(qst, kc, nt, preferred_element_type=F32)
        s_n = lax.dot_general(qst, kn, nt, preferred_element_type=F32)
        rc = lax.broadcasted_iota(jnp.int32, (rows, l_len), 1) - trow
        rn = trow - lax.broadcasted_iota(jnp.int32, (rows, t_new), 1)
        ok_c, ok_n = rc >= 0, rn >= 0
        if dil > 1:
            ok_c = ok_c & ((rc & (dil - 1)) == 0)
            ok_n = ok_n & ((rn & (dil - 1)) == 0)
        s_c = jnp.where(ok_c, s_c, NEG)
        s_n = jnp.where(ok_n, s_n, NEG)
        if j == 0:
            sink = jnp.where(hrow == 0, sink_ref[0],
                             jnp.where(hrow == 1, sink_ref[1], jnp.where(hrow == 2, sink_ref[2], sink_ref[3])))
        else:
            sink = jnp.full((rows, 1), -jnp.inf, F32)
        m = jnp.maximum(jnp.maximum(jnp.max(s_c, axis=-1, keepdims=True),
                                    jnp.max(s_n, axis=-1, keepdims=True)), sink)
        p_c = jnp.exp(s_c - m)
        p_n = jnp.exp(s_n - m)
        den = jnp.sum(p_c, axis=-1, keepdims=True) + jnp.sum(p_n, axis=-1, keepdims=True) + jnp.exp(sink - m)
        o = (_dot(p_c.astype(BF16), vc) + _dot(p_n.astype(BF16), vn)) / den
        lse = jnp.broadcast_to(m + jnp.log(den), (rows, LANES))

        def assemble(a):
            h0, h1, h2, h3 = (a[i * t_new:(i + 1) * t_new] for i in range(4))
            left = jnp.where(lt64, h0, pltpu.roll(h1, HEAD_DIM, 1))
            right = jnp.where(lt64, pltpu.roll(h2, HEAD_DIM, 1), h3)
            return jnp.concatenate([left, right], axis=1)

        outs.append(assemble(o))
        lses.append(assemble(lse))
    yb_ref[0] = outs[0]
    mx = jnp.maximum(jnp.maximum(lses[1], lses[2]), lses[3])
    ws = [jnp.exp(l - mx) for l in lses[1:]]
    yc_ref[0] = (ws[0] * outs[1] + ws[1] * outs[2] + ws[2] * outs[3]) / (ws[0] + ws[1] + ws[2])


def _sample_attn_call(sink, q, kvc, caches):
    n, t_new, _ = q.shape
    out = jax.ShapeDtypeStruct((n, t_new, GROUP_WIDTH), F32)
    row = lambda i: (i, 0, 0)
    return pl.pallas_call(
        functools.partial(_sample_attn_kernel, t_new=t_new),
        out_shape=(out, out),
        grid=(n,),
        in_specs=[pl.BlockSpec(memory_space=pltpu.SMEM),
                  pl.BlockSpec((1, t_new, 1024), row),
                  pl.BlockSpec((1, t_new, 1024), row)]
                 + [pl.BlockSpec((1, c.shape[1], GROUP_WIDTH), row) for c in caches],
        out_specs=(pl.BlockSpec((1, t_new, GROUP_WIDTH), row),) * 2,
        compiler_params=_cparams(("arbitrary",)),
    )(sink, q, kvc, *caches)


def _conv_tail(acc, b, gn, gm):
    y = acc + b
    yn = y * lax.rsqrt(_group_mean(y * y, gm) + EPS) * gn
    return yn * _sigmoid(yn)


def _conv_prompt_kernel(prev_ref, cur_ref, w_ref, b_ref, gn_ref, gm_ref, o_ref, buf_ref, *, tm, halo):
    first = pl.program_id(1) == 0
    buf_ref[0:halo, :] = jnp.where(first, 0.0, prev_ref[0])
    buf_ref[halo:halo + tm, :] = cur_ref[0]
    off = halo - (CONV_W - 1)
    acc = jnp.zeros((tm, GROUP_WIDTH), F32)
    for w in range(CONV_W):
        acc = acc + buf_ref[off + w:off + w + tm, :] * w_ref[w:w + 1, :]
    o_ref[0] = _conv_tail(acc, b_ref[...], gn_ref[...], gm_ref[...])


def _conv_prompt_call(glu, lw, *, tm):
    n, s_len, _ = glu.shape
    halo = 32
    per = tm // halo
    cst = lambda shape: pl.BlockSpec(shape, lambda b, i: (0, 0))
    return pl.pallas_call(
        functools.partial(_conv_prompt_kernel, tm=tm, halo=halo),
        out_shape=jax.ShapeDtypeStruct((n, s_len, GROUP_WIDTH), F32),
        grid=(n, s_len // tm),
        in_specs=[
            pl.BlockSpec((1, halo, GROUP_WIDTH), lambda b, i: (b, jnp.maximum(i * per - 1, 0), 0)),
            pl.BlockSpec((1, tm, GROUP_WIDTH), lambda b, i: (b, i, 0)),
            cst((32, GROUP_WIDTH)), cst((1, GROUP_WIDTH)), cst((1, GROUP_WIDTH)),
            cst((GROUP_WIDTH, GROUP_WIDTH)),
        ],
        out_specs=pl.BlockSpec((1, tm, GROUP_WIDTH), lambda b, i: (b, i, 0)),
        scratch_shapes=[pltpu.VMEM((tm + halo, GROUP_WIDTH), F32)],
        compiler_params=_cparams(("arbitrary", "arbitrary")),
    )(glu, glu, lw["conv_w"], lw["conv_b"], lw["d_gn"], lw["gm"])


def _conv_sample_kernel(cin_ref, w_ref, b_ref, gn_ref, gm_ref, o_ref, *, nb, t_new):
    acc = jnp.zeros((nb, t_new, GROUP_WIDTH), F32)
    for w in range(CONV_W):
        acc = acc + cin_ref[:, w:w + t_new, :] * w_ref[w:w + 1, :]
    y = _conv_tail(acc.reshape(nb * t_new, GROUP_WIDTH), b_ref[...], gn_ref[...], gm_ref[...])
    o_ref[...] = y.reshape(nb, t_new, GROUP_WIDTH)


def _conv_sample_call(conv_in, lw, *, nb, t_new):
    n, rows, _ = conv_in.shape
    cst = lambda shape: pl.BlockSpec(shape, lambda i: (0, 0))
    return pl.pallas_call(
        functools.partial(_conv_sample_kernel, nb=nb, t_new=t_new),
        out_shape=jax.ShapeDtypeStruct((n, t_new, GROUP_WIDTH), F32),
        grid=(n // nb,),
        in_specs=[pl.BlockSpec((nb, rows, GROUP_WIDTH), lambda i: (i, 0, 0)),
                  cst((32, GROUP_WIDTH)), cst((1, GROUP_WIDTH)), cst((1, GROUP_WIDTH)),
                  cst((GROUP_WIDTH, GROUP_WIDTH))],
        out_specs=pl.BlockSpec((nb, t_new, GROUP_WIDTH), lambda i: (i, 0, 0)),
        compiler_params=_cparams(("arbitrary",)),
    )(conv_in, lw["conv_w"], lw["conv_b"], lw["d_gn"], lw["gm"])


def _post_kernel(*refs, nb, tt, combine):
    if combine:
        (x_ref, g_ref, ya_ref, yb_ref, o0, o1, o2, l0, l1, l2, yd_ref, w_ref, out_ref) = refs
        a0, a1, a2 = l0[...], l1[...], l2[...]
        mx = jnp.maximum(jnp.maximum(a0, a1), a2)
        e0, e1, e2 = jnp.exp(a0 - mx), jnp.exp(a1 - mx), jnp.exp(a2 - mx)
        yc = (e0 * o0[...] + e1 * o1[...] + e2 * o2[...]) / (e0 + e1 + e2)
    else:
        (x_ref, g_ref, ya_ref, yb_ref, yc_ref, yd_ref, w_ref, out_ref) = refs
        yc = yc_ref[...]
    tm = nb * tt
    acc = jnp.zeros((tm, D_MODEL), F32)
    for idx, part in enumerate((ya_ref[...], yb_ref[...], yc, yd_ref[...])):
        pb = part.reshape(tm, GROUP_WIDTH).astype(BF16)
        acc = acc + _dot(pb, w_ref[idx * GROUP_WIDTH:(idx + 1) * GROUP_WIDTH, :])
    out_ref[...] = x_ref[...] + g_ref[...] * acc.reshape(nb, tt, D_MODEL)


def _post_call(x, mod, parts, w_out, *, nb, tt, combine):
    n, t_len, _ = x.shape
    ntt, tok, modm = _tile_maps(t_len, tt)
    grid = ((n // nb) * ntt,)
    in_specs = ([pl.BlockSpec((nb, tt, D_MODEL), tok(0)), pl.BlockSpec((nb, 1, D_MODEL), modm(2))]
                + [pl.BlockSpec((nb, tt, GROUP_WIDTH), tok(0)) for _ in parts]
                + [pl.BlockSpec((D_MODEL, D_MODEL), lambda i: (0, 0))])
    return pl.pallas_call(
        functools.partial(_post_kernel, nb=nb, tt=tt, combine=combine),
        out_shape=jax.ShapeDtypeStruct(x.shape, F32),
        grid=grid, in_specs=in_specs,
        out_specs=pl.BlockSpec((nb, tt, D_MODEL), tok(0)),
        compiler_params=_cparams(("arbitrary",)),
    )(x, mod, *parts, w_out)


def _top_values(s, k):
    vals = []
    for _ in range(k):
        m = jnp.max(s, axis=0, keepdims=True)
        vals.append(m)
        s = jnp.where(s == m, -jnp.inf, s)
    return vals


def _route_kernel(x_ref, sh_ref, sc_ref, g_ref, wqt_ref, sk_ref, ht_ref, a_ref, b_ref, c_ref, qt_scr,
                  *, nb, tt):
    tm = nb * tt
    h = _norm_mod(x_ref[...], g_ref[...], sc_ref[...], sh_ref[...]).reshape(tm, D_MODEL)
    htb = h.T.astype(BF16)
    ht_ref[...] = htb
    qt_scr[...] = _dot(wqt_ref[...], htb)
    half = PEER_NKEYS // 2

    def head(hh, carry):
        base = pl.multiple_of(hh * PEER_NKEYS, PEER_NKEYS)
        for cc in range(tm // LANES):
            cs = slice(cc * LANES, (cc + 1) * LANES)
            q1 = qt_scr[pl.ds(base, half), cs].astype(BF16)
            q2 = qt_scr[pl.ds(base + half, half), cs].astype(BF16)
            s1 = _dot(sk_ref[0], q1)
            s2 = _dot(sk_ref[1], q2)
            t1 = _top_values(s1, PEER_TOPK)
            t2 = _top_values(s2, PEER_TOPK)
            s1m = jnp.where(s1 >= t1[-1], s1, -jnp.inf)
            s2m = jnp.where(s2 >= t2[-1], s2, -jnp.inf)
            t2a = jnp.concatenate(t2, axis=0)
            cand = jnp.concatenate([t1[i] + t2a for i in range(PEER_TOPK)], axis=0)
            tv = _top_values(cand, PEER_TOPK + 1)
            mx = tv[0]
            z = tv[0] - mx
            z = jnp.exp(z)
            for i in range(1, PEER_TOPK):
                z = z + jnp.exp(tv[i] - mx)
            shift = -mx - jnp.log(z)
            a_ref[hh, :, cs] = s1m + shift
            b_ref[hh, :, cs] = s2m
            thr = 0.5 * (tv[PEER_TOPK - 1] + tv[PEER_TOPK]) + shift
            c_ref[hh, :, cs] = jnp.broadcast_to(thr, (SUBLANES, LANES))
        return carry

    lax.fori_loop(0, PEER_HEADS, head, 0)


def _route_call(x, mod, lw, *, nb, tt):
    n, t_len, _ = x.shape
    ntt, tok, modm = _tile_maps(t_len, tt)
    tm = nb * tt
    m_tot = n * t_len
    grid = ((n // nb) * ntt,)
    return pl.pallas_call(
        functools.partial(_route_kernel, nb=nb, tt=tt),
        out_shape=(jax.ShapeDtypeStruct((D_MODEL, m_tot), BF16),
                   jax.ShapeDtypeStruct((PEER_HEADS, PEER_NKEYS, m_tot), F32),
                   jax.ShapeDtypeStruct((PEER_HEADS, PEER_NKEYS, m_tot), F32),
                   jax.ShapeDtypeStruct((PEER_HEADS, SUBLANES, m_tot), F32)),
        grid=grid,
        in_specs=[pl.BlockSpec((nb, tt, D_MODEL), tok(0)),
                  pl.BlockSpec((nb, 1, D_MODEL), modm(3)),
                  pl.BlockSpec((nb, 1, D_MODEL), modm(4)),
                  pl.BlockSpec((1, D_MODEL), lambda i: (0, 0)),
                  pl.BlockSpec((D_MODEL, D_MODEL), lambda i: (0, 0)),
                  pl.BlockSpec((2, PEER_NKEYS, PEER_NKEYS // 2), lambda i: (0, 0, 0))],
        out_specs=(pl.BlockSpec((D_MODEL, tm), lambda i: (0, i)),
                   pl.BlockSpec((PEER_HEADS, PEER_NKEYS, tm), lambda i: (0, 0, i)),
                   pl.BlockSpec((PEER_HEADS, PEER_NKEYS, tm), lambda i: (0, 0, i)),
                   pl.BlockSpec((PEER_HEADS, SUBLANES, tm), lambda i: (0, 0, i))),
        scratch_shapes=[pltpu.VMEM((D_MODEL, tm), F32)],
        compiler_params=_cparams(("arbitrary",)),
    )(x, mod, mod, lw["g_ffn"], lw["wqt"], lw["subkeys"])


def _peer_kernel(ht_ref, a_ref, b_ref, c_ref, u_ref, vt_ref, x_ref, g_ref, out_ref, yt_scr, st_scr, wt_scr, arow_scr,
                 *, nb, tt, eb):
    j = pl.program_id(1)
    tm = nb * tt

    @pl.when(j == 0)
    def _():
        yt_scr[...] = jnp.zeros_like(yt_scr)

    st_scr[...] = _dot(u_ref[...], ht_ref[...])
    n_first = eb // PEER_NKEYS
    for hh in range(PEER_HEADS):
        for r in range(n_first):
            arow_scr[hh * n_first + r] = jnp.broadcast_to(a_ref[hh, r:r + 1, :], (SUBLANES, tm))
    tiles = PEER_NKEYS // SUBLANES

    def first_key(ii, carry):
        r0 = pl.multiple_of(ii * PEER_NKEYS, PEER_NKEYS)
        for cc in range(tm // LANES):
            cs = slice(cc * LANES, (cc + 1) * LANES)
            gate = jnp.zeros((tiles, SUBLANES, LANES), F32)
            for hh in range(PEER_HEADS):
                arow = arow_scr[hh * n_first + ii][:, cs]
                u = arow[None] + b_ref[hh, :, cs].reshape(tiles, SUBLANES, LANES)
                gate = gate + jnp.where(u >= c_ref[hh, :, cs][None], jnp.exp(u), 0.0)
            s = st_scr[pl.ds(r0, PEER_NKEYS), cs]
            w = gate.reshape(PEER_NKEYS, LANES) * _gelu(s)
            wt_scr[pl.ds(r0, PEER_NKEYS), cs] = w.astype(BF16)
        return carry

    lax.fori_loop(0, n_first, first_key, 0)
    yt_scr[...] += _dot(vt_ref[...], wt_scr[...])

    @pl.when(j == pl.num_programs(1) - 1)
    def _():
        y = yt_scr[...].T.reshape(nb, tt, D_MODEL)
        out_ref[...] = x_ref[...] + g_ref[...] * y


def _peer_call(x, mod, ht, a, b, c, lw, *, nb, tt, eb):
    n, t_len, _ = x.shape
    ntt, _, _ = _tile_maps(t_len, tt)
    tm = nb * tt
    n_exp = lw["u"].shape[0]
    grid = ((n // nb) * ntt, n_exp // eb)
    tok = lambda i, j: (i // ntt, i % ntt, 0)
    return pl.pallas_call(
        functools.partial(_peer_kernel, nb=nb, tt=tt, eb=eb),
        out_shape=jax.ShapeDtypeStruct(x.shape, F32),
        grid=grid,
        in_specs=[pl.BlockSpec((D_MODEL, tm), lambda i, j: (0, i)),
                  pl.BlockSpec((PEER_HEADS, eb // PEER_NKEYS, tm), lambda i, j: (0, j, i)),
                  pl.BlockSpec((PEER_HEADS, PEER_NKEYS, tm), lambda i, j: (0, 0, i)),
                  pl.BlockSpec((PEER_HEADS, SUBLANES, tm), lambda i, j: (0, 0, i)),
                  pl.BlockSpec((eb, D_MODEL), lambda i, j: (j, 0)),
                  pl.BlockSpec((D_MODEL, eb), lambda i, j: (0, j)),
                  pl.BlockSpec((nb, tt, D_MODEL), tok),
                  pl.BlockSpec((nb, 1, D_MODEL), lambda i, j: (i // ntt, 0, 5))],
        out_specs=pl.BlockSpec((nb, tt, D_MODEL), tok),
        scratch_shapes=[pltpu.VMEM((D_MODEL, tm), F32),
                        pltpu.VMEM((eb, tm), F32),
                        pltpu.VMEM((eb, tm), BF16),
                        pltpu.VMEM((PEER_HEADS * (eb // PEER_NKEYS), SUBLANES, tm), F32)],
        compiler_params=_cparams(("arbitrary", "arbitrary")),
    )(ht, a, b, c, lw["u"], lw["vt"], x, mod)


def _rope_tables(pos):
    half = ROT_DIM // 2
    inv = ROPE_THETA ** (-jnp.arange(half, dtype=F32) * 2.0 / ROT_DIM)
    ang = pos.astype(F32)[:, None] * inv[None, :]
    cos, sin = jnp.cos(ang), jnp.sin(ang)
    t = pos.shape[0]
    rest = HEAD_DIM - ROT_DIM
    cc = jnp.concatenate([cos, cos, jnp.ones((t, rest), F32)], axis=1)
    sa = jnp.concatenate([-sin, jnp.zeros((t, HEAD_DIM - half), F32)], axis=1)
    sb = jnp.concatenate([jnp.zeros((t, half), F32), sin, jnp.zeros((t, rest), F32)], axis=1)
    reps = GROUP_WIDTH // HEAD_DIM
    return tuple(jnp.tile(a, (1, reps)) for a in (cc, sa, sb))


def _dup_heads(w, n_pairs):
    lead = w.shape[:-1]
    w = w.reshape(lead + (n_pairs * 2, 1, HEAD_DIM))
    return jnp.broadcast_to(w, lead + (n_pairs * 2, 2, HEAD_DIM)).reshape(lead + (n_pairs * 4 * HEAD_DIM,))


def _layer_weights(l, w_in, w_out, g_mix, g_a_v, a_ws, a_bs, b_gq, b_gk, c_gq, c_gk, d_conv_w, d_conv_b,
                   d_gn, g_ffn, peer_wq, peer_subkeys, peer_u, peer_v):
    wi = w_in[l]
    a_u, a_v, b_q, b_k, b_v, c_q, c_k, c_v, d_a, d_g = jnp.split(
        wi, np.cumsum((256, 256, 256, 128, 128, 768, 384, 384, 256)).tolist(), axis=1)
    k_all = jnp.concatenate([b_k, c_k], axis=1)
    v_all = jnp.concatenate([b_v, c_v], axis=1)
    w_perm = jnp.concatenate([a_u, a_v, b_q, c_q, _dup_heads(k_all, 4), _dup_heads(v_all, 4), d_a, d_g], axis=1)
    gq = jnp.concatenate([jnp.tile(b_gq[l], 4)] + [jnp.tile(c_gq[l, g], 4) for g in range(3)])[None, :]
    gk = jnp.concatenate([jnp.tile(b_gk[l], 4)] + [jnp.tile(c_gk[l, g], 4) for g in range(3)])[None, :]
    grp = np.arange(GROUP_WIDTH) // HEAD_DIM
    gm = jnp.asarray((grp[:, None] == grp[None, :]) / HEAD_DIM, BF16)
    hmask = jnp.asarray(np.arange(4)[:, None] == grp[None, :], F32)
    conv_w = jnp.concatenate([d_conv_w[l], jnp.zeros((1, GROUP_WIDTH), F32)], axis=0)
    return dict(
        w_in=w_perm.astype(BF16), w_out=w_out[l].astype(BF16), g_mix=g_mix[l][None, :],
        g_a_v=g_a_v[l][None, :], gq=gq, gk=gk, gm=gm, hmask=hmask,
        a_ws=a_ws[l], a_bs=a_bs[l],
        conv_w=conv_w, conv_b=d_conv_b[l][None, :], d_gn=d_gn[l][None, :],
        g_ffn=g_ffn[l][None, :], wqt=peer_wq[l].T.astype(BF16), subkeys=peer_subkeys[l].astype(BF16),
        u=peer_u[l].astype(BF16), vt=peer_v[l].T.astype(BF16))


def _mix_tables(a_ws, a_bs, chunk_len):
    reps = CHUNK // chunk_len
    tri = jnp.tril(jnp.ones((chunk_len, chunk_len), F32))
    blk = a_ws[:, :chunk_len, :chunk_len] * tri
    eye = jnp.eye(reps, dtype=F32)
    full = jnp.einsum("ab,hij->haibj", eye, blk).reshape(4, CHUNK, CHUNK)
    mixw = jnp.concatenate([full[h] for h in range(4)], axis=1).astype(BF16)
    bias = jnp.tile(a_bs[:, :chunk_len].T, (reps, 1))
    mixb = jnp.repeat(bias, HEAD_DIM, axis=1)
    return mixw, mixb


def _deinterleave(a, dil):
    n, s_len, c = a.shape
    return a.reshape(n, s_len // dil, dil, c).transpose(0, 2, 1, 3).reshape(n * dil, s_len // dil, c)


def _reinterleave(a, dil):
    nd, l_len, c = a.shape
    n = nd // dil
    return a.reshape(n, dil, l_len, c).transpose(0, 2, 1, 3).reshape(n, l_len * dil, c)


PROMPT_TT = 512
SAMPLE_NB = 64
ATTN_TQ = 512
PEER_EB = 1024
ROUTE_TM = 256


def _prompt_layer(x, mod, lw, sink, rope):
    s_len = x.shape[1]
    tt = min(PROMPT_TT, s_len)
    no_sink = jnp.full((4,), -jnp.inf, F32)
    lw = dict(lw)
    lw["mixw"], lw["mixb"] = _mix_tables(lw["a_ws"], lw["a_bs"], CHUNK)
    ya, _, q, kd, vd, kvc, glu = _pre_call(x, mod, lw, rope, nb=1, tt=tt, rope_by_tile=True)
    yb, _ = _band_attn_call(sink, q, kd, vd, 0, tq=min(ATTN_TQ, s_len))
    o_c, lse_c = [], []
    for g, (_, dil) in enumerate(C_PAIRS):
        if dil == 1:
            o, lse = _band_attn_call(no_sink, q, kd, vd, g + 1, tq=min(ATTN_TQ, s_len))
        else:
            sl = slice((g + 1) * GROUP_WIDTH, (g + 2) * GROUP_WIDTH)
            qg, kg, vg = (_deinterleave(a[:, :, sl], dil) for a in (q, kd, vd))
            o, lse = _band_attn_call(no_sink, qg, kg, vg, 0, tq=min(ATTN_TQ, s_len // dil))
            o, lse = _reinterleave(o, dil), _reinterleave(lse, dil)
        o_c.append(o)
        lse_c.append(lse)
    yd = _conv_prompt_call(glu, lw, tm=tt)
    x1 = _post_call(x, mod, [ya, yb] + o_c + lse_c + [yd], lw["w_out"], nb=1, tt=tt, combine=True)
    ht, ra, rb, rc = _route_call(x1, mod, lw, nb=1, tt=ROUTE_TM)
    y = _peer_call(x1, mod, ht, ra, rb, rc, lw, nb=1, tt=tt, eb=PEER_EB)
    st = [kvc[:, -min(BLOCK, s_len):, 0:GROUP_WIDTH]]
    for g, (win, _) in enumerate(C_PAIRS):
        st.append(kvc[:, -min(win, s_len):, (g + 1) * GROUP_WIDTH:(g + 2) * GROUP_WIDTH])
    st = [a.reshape(a.shape[0], a.shape[1], 2, 2, HEAD_DIM) for a in st]
    st.append(glu[:, -(CONV_W - 1):])
    return y, st


def _sample_layer(x, mod, lw, sink, rope, caches, state):
    n_s, t_new, _ = x.shape
    lw = dict(lw)
    lw["mixw"], lw["mixb"] = _mix_tables(lw["a_ws"], lw["a_bs"], t_new)
    ya, av, q, _, _, kvc, glu = _pre_call(x, mod, lw, rope, nb=SAMPLE_NB, tt=t_new, rope_by_tile=False)
    caches = [c.reshape(n_s, c.shape[1], GROUP_WIDTH) for c in caches]
    yb, yc = _sample_attn_call(sink, q, kvc, caches)
    conv_in = jnp.concatenate([state, glu, jnp.zeros((n_s, 2, GROUP_WIDTH), F32)], axis=1)
    yd = _conv_sample_call(conv_in, lw, nb=SAMPLE_NB, t_new=t_new)
    x1 = _post_call(x, mod, [ya, yb, yc, yd], lw["w_out"], nb=SAMPLE_NB, tt=t_new, combine=False)
    ht, ra, rb, rc = _route_call(x1, mod, lw, nb=ROUTE_TM // t_new, tt=t_new)
    y = _peer_call(x1, mod, ht, ra, rb, rc, lw, nb=SAMPLE_NB, tt=t_new, eb=PEER_EB)
    st = [kvc[:, :, g * GROUP_WIDTH:(g + 1) * GROUP_WIDTH].reshape(n_s, t_new, 2, 2, HEAD_DIM)
          for g in range(4)]
    st.append(jnp.concatenate([state, glu], axis=1)[:, -(CONV_W - 1):])
    st.append(av.reshape(n_s, t_new, 4, HEAD_DIM))
    return y, st


def kernel(x_prompt, x_sample, cache_b_kv, cache_c1_kv, cache_c2_kv, cache_c3_kv, state_d_conv, c_prompt, c_sample, w_ada, b_ada, g_mix, w_in, w_out, g_a_v, a_ws, a_bs, b_gq, b_gk, b_sink, c_gq, c_gk, d_conv_w, d_conv_b, d_gn, g_ffn, peer_wq, peer_subkeys, peer_u, peer_v):
    n_p, s_len, _ = x_prompt.shape
    n_s, t_new, _ = x_sample.shape

    n_c = n_p + n_s
    pad = (-n_c) % 8
    c_all = jnp.concatenate([c_prompt, c_sample, jnp.zeros((pad, D_MODEL), F32)], axis=0)
    mod_all = _ada_call(c_all, w_ada.astype(BF16), b_ada[:, None, :])

    rope_p = _rope_tables(jnp.arange(s_len, dtype=jnp.int32))
    rope_s = tuple(jnp.tile(a, (SAMPLE_NB, 1))
                   for a in _rope_tables(PAST_LEN + jnp.arange(t_new, dtype=jnp.int32)))

    yp, ys = x_prompt, x_sample
    outs_p, outs_s = [], []
    for l in range(DEPTH):
        lw = _layer_weights(l, w_in, w_out, g_mix, g_a_v, a_ws, a_bs, b_gq, b_gk, c_gq, c_gk, d_conv_w,
                            d_conv_b, d_gn, g_ffn, peer_wq, peer_subkeys, peer_u, peer_v)
        yp, st = _prompt_layer(yp, mod_all[l, :n_p][:, None, :], lw, b_sink[l], rope_p)
        outs_p.append(st)
        caches = [c[l] for c in (cache_b_kv, cache_c1_kv, cache_c2_kv, cache_c3_kv)]
        ys, st = _sample_layer(ys, mod_all[l, n_p:n_c][:, None, :], lw, b_sink[l], rope_s, caches,
                               state_d_conv[l])
        outs_s.append(st)

    b_p, c1_p, c2_p, c3_p, d_p = (jnp.stack([s[i] for s in outs_p]) for i in range(5))
    b_s, c1_s, c2_s, c3_s, d_s, a_s = (jnp.stack([s[i] for s in outs_s]) for i in range(6))
    return (yp, ys, b_p, b_s, c1_p, c1_s, c2_p, c2_s, c3_p, c3_s, d_p, d_s, a_s)
```
